```python
import math
import jax
import jax.numpy as jnp
from jax import lax
import numpy as np

D_MODEL = 1024
BATCH = 4
SEQ = 4096
DEPTH = 2

CTX_LEN = 256
GRID_W = 64
D_MIX = 2 * D_MODEL
W_BR = D_MIX // 4
NORM_EPS = 1e-6

HY_ORDER = 2
HY_EMB = 33
HY_BANDS = (HY_EMB - 1) // 2
HY_HID = 64
HY_SHORT = 3
HY_TARGET = 1e-2
HY_MIN_DECAY = math.log(HY_TARGET) / 1.5
HY_MAX_DECAY = math.log(HY_TARGET) / 0.3
HY_EPS = 1e-6

RG_HEADS = 8
RG_HD = W_BR // RG_HEADS
RG_C = 8.0
RG_CONV = 4

HG_DK = 128
HG_HEADS = W_BR // HG_DK
HG_CHUNK = 64

M2_HD = 64
M2_HEADS = W_BR // M2_HD
M2_GROUPS = 2
M2_STATE = 128
M2_CONV = 4
M2_CHUNK = 64
M2_XBC = W_BR + 2 * M2_GROUPS * M2_STATE

PIECES = (3 * W_BR, W_BR, W_BR, W_BR, W_BR, W_BR, W_BR, W_BR, W_BR, M2_XBC, M2_HEADS, W_BR)
IN_COLS = sum(PIECES)
SPLIT_IDX = tuple(int(v) for v in np.cumsum(PIECES)[:-1])

kernel_name = 'hybrid_hyena_rglru_hgrn2_ssd_prefix'


def rmsnorm(x, w):
    xf = x.astype(jnp.float32)
    y = xf * lax.rsqrt(jnp.mean(xf * xf, axis=-1, keepdims=True) + NORM_EPS)
    return (y * w.astype(jnp.float32)).astype(x.dtype)


def dwconv(x, w, b, pad_lo, pad_hi):
    y = lax.conv_general_dilated(x, w.astype(x.dtype)[:, None, :], window_strides=(1,),
                                 padding=[(pad_lo, pad_hi)],
                                 dimension_numbers=('NWC', 'WIO', 'NWC'),
                                 feature_group_count=x.shape[-1])
    return y + b.astype(x.dtype)


def to_colmajor(a, rows):
    b, n, ch = a.shape
    return a.reshape(b, rows, GRID_W, ch).transpose(0, 2, 1, 3).reshape(b, n, ch)


def from_colmajor(a, rows):
    b, n, ch = a.shape
    return a.reshape(b, GRID_W, rows, ch).transpose(0, 2, 1, 3).reshape(b, n, ch)


def bidir(dir_fn, ctx_fwd, lat_fwd, ctx_bwd, lat_bwd, state0, p_fwd, p_bwd):
    rev = lambda t: tuple(jnp.flip(a, 1) for a in t)
    yc_f, s_f = dir_fn(*ctx_fwd, state0, *p_fwd)
    yl_f, _ = dir_fn(*lat_fwd, s_f, *p_fwd)
    yc_b, s_b = dir_fn(*rev(ctx_bwd), state0, *p_bwd)
    yl_b, _ = dir_fn(*rev(lat_bwd), s_b, *p_bwd)
    return yc_f + jnp.flip(yc_b, 1), yl_f + jnp.flip(yl_b, 1)


def hyena_filters(n, w1, b1, w2, b2, w3, freq):
    f32 = jnp.float32
    t = jnp.linspace(0.0, 1.0, n, dtype=f32)[:, None]
    bands = jnp.linspace(1e-4, HY_BANDS - 1, HY_BANDS, dtype=f32)[None]
    ang = (2.0 * math.pi / n) * jnp.arange(n, dtype=f32)[:, None] * bands
    z = jnp.concatenate([t, jnp.cos(ang), -jnp.sin(ang)], axis=-1)
    fr = freq.astype(f32)
    h = jnp.sin(fr * (z @ w1.astype(f32) + b1.astype(f32)))
    h = jnp.sin(fr * (h @ w2.astype(f32) + b2.astype(f32)))
    h = (h @ w3.astype(f32)).reshape(n, HY_ORDER, 2, W_BR)
    deltas = jnp.abs(jnp.linspace(HY_MIN_DECAY, HY_MAX_DECAY, W_BR, dtype=f32))
    return h * jnp.exp(-t[:, :, None, None] * deltas)


def two_sided_kernel(h_fwd, h_bwd):
    k = jnp.concatenate([h_fwd, jnp.zeros_like(h_fwd[:1]), jnp.flip(h_bwd[1:], 0)], axis=0)
    return k / (jnp.sum(jnp.abs(k), axis=0, keepdims=True) + HY_EPS)


def fft_conv(u, k):
    n = u.shape[1]
    y = jnp.fft.irfft(jnp.fft.rfft(u, n=2 * n, axis=1) * jnp.fft.rfft(k, axis=0), n=2 * n, axis=1)
    return y[:, :n]


def hyena_mix(u, conv_w, conv_b, w1, b1, w2, b2, w3, freq, skip):
    n = u.shape[1]
    u = dwconv(u, conv_w, conv_b, HY_SHORT // 2, HY_SHORT // 2).astype(jnp.float32)
    v, x1, x2 = jnp.split(u, 3, axis=-1)
    h = hyena_filters(n, w1, b1, w2, b2, w3, freq)
    z = v
    for o, xg in enumerate((x1, x2)):
        k = two_sided_kernel(h[:, o, 0], h[:, o, 1])
        z = xg * (fft_conv(z, k) + skip[o].astype(jnp.float32) * z)
    return z


def _lin_combine(e1, e2):
    a1, b1 = e1
    a2, b2 = e2
    return a1 * a2, a2 * b1 + b2


def rglru_dir(x, h0, conv_w, conv_b, wa, ba, wx, bx, lam):
    f32 = jnp.float32
    bn, n, _ = x.shape
    xc = dwconv(x, conv_w, conv_b, RG_CONV - 1, 0).astype(f32)
    xh = xc.reshape(bn, n, RG_HEADS, RG_HD)
    r = jax.nn.sigmoid(jnp.einsum('blhi,hij->blhj', xh, wa.astype(f32)).reshape(bn, n, W_BR) + ba.astype(f32))
    gi = jax.nn.sigmoid(jnp.einsum('blhi,hij->blhj', xh, wx.astype(f32)).reshape(bn, n, W_BR) + bx.astype(f32))
    log_a = -RG_C * r * jax.nn.softplus(-lam.astype(f32))
    a = jnp.exp(log_a)
    b = jnp.sqrt(-jnp.expm1(2.0 * log_a)) * (gi * xc)
    a_cum, h = lax.associative_scan(_lin_combine, (a, b), axis=1)
    h = h + a_cum * h0[:, None]
    return h, h[:, -1]


def gla_chunked(q, k, v, log_f, s0):
    bn, n, nh, _ = q.shape
    dv = v.shape[-1]
    nc = n // HG_CHUNK
    to_chunks = lambda a: a.reshape(bn, nc, HG_CHUNK, nh, a.shape[-1]).transpose(1, 0, 3, 2, 4)
    mask = jnp.tril(jnp.ones((HG_CHUNK, HG_CHUNK), dtype=bool))[:, :, None]

    def step(s, inp):
        qc, kc, vc, lf = inp
        g = jnp.cumsum(lf, axis=2)
        o_inter = jnp.einsum('bhtd,bhde->bhte', qc * jnp.exp(g), s)
        diff = g[:, :, :, None, :] - g[:, :, None, :, :]
        decay = jnp.exp(jnp.where(mask, diff, -jnp.inf))
        att = jnp.einsum('bhtd,bhsd,bhtsd->bhts', qc, kc, decay)
        o = o_inter + jnp.einsum('bhts,bhse->bhte', att, vc)
        g_last = g[:, :, -1]
        s = jnp.exp(g_last)[..., None] * s + jnp.einsum('bhsd,bhse->bhde', kc * jnp.exp(g_last[:, :, None] - g), vc)
        return s, o

    s_last, o = lax.scan(step, s0, (to_chunks(q), to_chunks(k), to_chunks(v), to_chunks(log_f)))
    return o.transpose(1, 0, 3, 2, 4).reshape(bn, n, nh, dv), s_last


def hgrn2_dir(q, f_logit, v, s0, lb):
    f32 = jnp.float32
    bn, n, _ = q.shape
    lb = lb.astype(f32)
    z = f_logit.astype(f32)
    f = lb + (1.0 - lb) * jax.nn.sigmoid(z)
    k = (1.0 - lb) * jax.nn.sigmoid(-z)
    heads = lambda a: a.reshape(bn, n, HG_HEADS, HG_DK)
    return gla_chunked(heads(q.astype(f32) * HG_DK ** -0.5), heads(k), heads(v.astype(f32)),
                       heads(jnp.log(f)), s0)


def segsum(a):
    cs = jnp.cumsum(a, axis=-1)
    t = a.shape[-1]
    mask = jnp.tril(jnp.ones((t, t), dtype=bool))
    return jnp.where(mask, cs[..., :, None] - cs[..., None, :], -jnp.inf)


def ssd_chunked(xdt, adt, bm, cm, h0):
    bn, n, nh, hp = xdt.shape
    nc = n // M2_CHUNK
    X = xdt.reshape(bn, nc, M2_CHUNK, nh, hp)
    Bc = bm.reshape(bn, nc, M2_CHUNK, nh, M2_STATE)
    Cc = cm.reshape(bn, nc, M2_CHUNK, nh, M2_STATE)
    A = adt.reshape(bn, nc, M2_CHUNK, nh).transpose(0, 3, 1, 2)
    A_cum = jnp.cumsum(A, axis=-1)
    y_diag = jnp.einsum('bclhn,bcshn,bhcls,bcshp->bclhp', Cc, Bc, jnp.exp(segsum(A)), X)
    decay_states = jnp.exp(A_cum[..., -1:] - A_cum)
    states = jnp.einsum('bclhn,bhcl,bclhp->bchpn', Bc, decay_states, X)
    states = jnp.concatenate([h0[:, None], states], axis=1)
    chunk_decay = jnp.exp(segsum(jnp.pad(A_cum[..., -1], ((0, 0), (0, 0), (1, 0)))))
    states = jnp.einsum('bhzc,bchpn->bzhpn', chunk_decay, states)
    y_off = jnp.einsum('bclhn,bchpn,bhcl->bclhp', Cc, states[:, :-1], jnp.exp(A_cum))
    return (y_diag + y_off).reshape(bn, n, nh, hp), states[:, -1]


def ssd_dir(xbc, dt_raw, h0, conv_w, conv_b, dt_bias, a_log, d_skip):
    f32 = jnp.float32
    bn, n, _ = xbc.shape
    xbc = jax.nn.silu(dwconv(xbc, conv_w, conv_b, M2_CONV - 1, 0).astype(f32))
    xs, bm, cm = jnp.split(xbc, [W_BR, W_BR + M2_GROUPS * M2_STATE], axis=-1)
    xs = xs.reshape(bn, n, M2_HEADS, M2_HD)
    rep = M2_HEADS // M2_GROUPS
    bm = jnp.repeat(bm.reshape(bn, n, M2_GROUPS, M2_STATE), rep, axis=2)
    cm = jnp.repeat(cm.reshape(bn, n, M2_GROUPS, M2_STATE), rep, axis=2)
    dt = jax.nn.softplus(dt_raw.astype(f32) + dt_bias.astype(f32))
    a = -jnp.exp(a_log.astype(f32))
    y, h_last = ssd_chunked(xs * dt[..., None], dt * a, bm, cm, h0)
    return y + d_skip.astype(f32)[:, None] * xs, h_last


def head_rmsnorm(o, w):
    o = o * lax.rsqrt(jnp.mean(o * o, axis=-1, keepdims=True) + NORM_EPS)
    return o.reshape(o.shape[0], o.shape[1], -1) * w.astype(jnp.float32)


def merge_branches(u, y_hy, y_rg, y_hg, y_m2, hg_norm_w, m2_norm_w, w_out):
    f32 = jnp.float32
    bn, n = u[0].shape[:2]
    gate = lambda g: jax.nn.silu(g.astype(f32))
    b_hy = y_hy * gate(u[1])
    b_rg = y_rg * gate(u[3])
    b_hg = head_rmsnorm(y_hg, hg_norm_w) * gate(u[8])
    m2 = (y_m2.reshape(bn, n, W_BR) * gate(u[11])).reshape(bn, n, M2_GROUPS, W_BR // M2_GROUPS)
    b_m2 = (m2 * lax.rsqrt(jnp.mean(m2 * m2, axis=-1, keepdims=True) + NORM_EPS)).reshape(bn, n, W_BR) * m2_norm_w.astype(f32)
    y = jnp.concatenate([b_hy, b_rg, b_hg, b_m2], axis=-1).astype(u[0].dtype)
    return y @ w_out


def setup_inputs(seed: int = 0) -> dict:
    key = jax.random.key(seed)
    ks = iter(jax.random.split(key, 48))
    f32 = jnp.float32
    nrm = lambda shape, scale: scale * jax.random.normal(next(ks), shape, f32)
    D, W, X, L = D_MODEL, W_BR, M2_XBC, DEPTH
    ac = jax.random.uniform(next(ks), (L, 2, W), f32, 0.9, 0.999)
    a = ac ** (1.0 / RG_C)
    rg_lam = jnp.log(a) - jnp.log1p(-a)
    dt = jnp.exp(jax.random.uniform(next(ks), (L, 2, M2_HEADS), f32, math.log(1e-3), math.log(1e-1)))
    m2_dt_bias = dt + jnp.log(-jnp.expm1(-dt))
    m2_a_log = jnp.log(jax.random.uniform(next(ks), (L, 2, M2_HEADS), f32, 1.0, 16.0))
    return {
        'x': nrm((BATCH, SEQ, D), 1.0),
        'c': nrm((BATCH, D), 1.0),
        'ctx': nrm((BATCH, CTX_LEN, D), 1.0),
        'c_ctx': nrm((D,), 1.0),
        'w_mod': nrm((L, D, 3 * D), 0.5 * D ** -0.5),
        'b_mod': nrm((L, 3 * D), 0.02),
        'norm_w': 1.0 + nrm((L, D), 0.02),
        'w_in': nrm((L, D, IN_COLS), D ** -0.5),
        'w_out': nrm((L, D_MIX, D), D_MIX ** -0.5),
        'hy_conv_w': nrm((L, HY_SHORT, 3 * W), HY_SHORT ** -0.5),
        'hy_conv_b': nrm((L, 3 * W), 0.02),
        'hy_w1': nrm((L, HY_EMB, HY_HID), HY_EMB ** -0.5),
        'hy_b1': nrm((L, HY_HID), 0.1),
        'hy_w2': nrm((L, HY_HID, HY_HID), HY_HID ** -0.5),
        'hy_b2': nrm((L, HY_HID), 0.1),
        'hy_w3': nrm((L, HY_HID, HY_ORDER * 2 * W), HY_HID ** -0.5),
        'hy_freq': 1.0 + nrm((L, HY_HID), 0.1),
        'hy_skip': nrm((L, HY_ORDER, W), 1.0),
        'rg_conv_w': nrm((L, 2, RG_CONV, W), RG_CONV ** -0.5),
        'rg_conv_b': nrm((L, 2, W), 0.02),
        'rg_wa': nrm((L, 2, RG_HEADS, RG_HD, RG_HD), RG_HD ** -0.5),
        'rg_ba': nrm((L, 2, W), 0.1),
        'rg_wx': nrm((L, 2, RG_HEADS, RG_HD, RG_HD), RG_HD ** -0.5),
        'rg_bx': nrm((L, 2, W), 0.1),
        'rg_lam': rg_lam,
        'hg_lb': nrm((L, 2, W), 1.0),
        'hg_norm_w': 1.0 + nrm((L, W), 0.02),
        'm2_conv_w': nrm((L, 2, M2_CONV, X), M2_CONV ** -0.5),
        'm2_conv_b': nrm((L, 2, X), 0.02),
        'm2_dt_bias': m2_dt_bias,
        'm2_a_log': m2_a_log,
        'm2_d': 1.0 + nrm((L, 2, M2_HEADS), 0.1),
        'm2_norm_w': 1.0 + nrm((L, W), 0.02),
        'final_norm_w': 1.0 + nrm((D,), 0.02),
    }


def reference(x, c, ctx, c_ctx, w_mod, b_mod, norm_w, w_in, w_out,
              hy_conv_w, hy_conv_b, hy_w1, hy_b1, hy_w2, hy_b2, hy_w3, hy_freq, hy_skip,
              rg_conv_w, rg_conv_b, rg_wa, rg_ba, rg_wx, rg_bx, rg_lam,
              hg_lb, hg_norm_w,
              m2_conv_w, m2_conv_b, m2_dt_bias, m2_a_log, m2_d, m2_norm_w,
              final_norm_w):
    f32 = jnp.float32
    bsz, n_lat, _ = x.shape
    rows = n_lat // GRID_W
    lb_all = jnp.cumsum(jax.nn.softmax(hg_lb.astype(f32), axis=0), axis=0)
    lb_all = lb_all - lb_all[:1]
    cond_l = jax.nn.silu(c)
    cond_c = jax.nn.silu(c_ctx)
    rg0 = jnp.zeros((bsz, W_BR), f32)
    hg0 = jnp.zeros((bsz, HG_HEADS, HG_DK, HG_DK), f32)
    m20 = jnp.zeros((bsz, M2_HEADS, M2_HD, M2_STATE), f32)
    xl, xc = x, ctx
    for i in range(DEPTH):
        need_ctx = i < DEPTH - 1
        sh_l, sc_l, g_l = jnp.split(cond_l @ w_mod[i] + b_mod[i], 3, axis=-1)
        sh_c, sc_c, g_c = jnp.split(cond_c @ w_mod[i] + b_mod[i], 3, axis=-1)
        hl = rmsnorm(xl, norm_w[i]) * (1.0 + sc_l[:, None]) + sh_l[:, None]
        hc = rmsnorm(xc, norm_w[i]) * (1.0 + sc_c) + sh_c
        ul = jnp.split(hl @ w_in[i], SPLIT_IDX, axis=-1)
        uc = jnp.split(hc @ w_in[i], SPLIT_IDX, axis=-1)

        hy_p = (hy_conv_w[i], hy_conv_b[i], hy_w1[i], hy_b1[i], hy_w2[i], hy_b2[i], hy_w3[i], hy_freq[i], hy_skip[i])
        lat_hy = hyena_mix(ul[0], *hy_p)

        rg_f = (rg_conv_w[i, 0], rg_conv_b[i, 0], rg_wa[i, 0], rg_ba[i, 0], rg_wx[i, 0], rg_bx[i, 0], rg_lam[i, 0])
        rg_b = (rg_conv_w[i, 1], rg_conv_b[i, 1], rg_wa[i, 1], rg_ba[i, 1], rg_wx[i, 1], rg_bx[i, 1], rg_lam[i, 1])
        ctx_rg, lat_rg = bidir(rglru_dir, (uc[2],), (ul[2],), (uc[2],), (ul[2],), rg0, rg_f, rg_b)

        ctx_hg, lat_hg = bidir(hgrn2_dir, (uc[4], uc[5], uc[7]), (ul[4], ul[5], ul[7]),
                               (uc[4], uc[6], uc[7]), (ul[4], ul[6], ul[7]), hg0,
                               (lb_all[i, 0],), (lb_all[i, 1],))

        m2_f = (m2_conv_w[i, 0], m2_conv_b[i, 0], m2_dt_bias[i, 0], m2_a_log[i, 0], m2_d[i, 0])
        m2_b = (m2_conv_w[i, 1], m2_conv_b[i, 1], m2_dt_bias[i, 1], m2_a_log[i, 1], m2_d[i, 1])
        m2_ctx_in = (uc[9], uc[10])
        m2_lat_in = (to_colmajor(ul[9], rows), to_colmajor(ul[10], rows))
        ctx_m2, lat_m2 = bidir(ssd_dir, m2_ctx_in, m2_lat_in, m2_ctx_in, m2_lat_in, m20, m2_f, m2_b)
        lat_m2 = from_colmajor(lat_m2.reshape(bsz, n_lat, W_BR), rows)

        xl = xl + g_l[:, None] * merge_branches(ul, lat_hy, lat_rg, lat_hg, lat_m2,
                                                hg_norm_w[i], m2_norm_w[i], w_out[i])
        if need_ctx:
            ctx_hy = hyena_mix(uc[0], *hy_p)
            xc = xc + g_c * merge_branches(uc, ctx_hy, ctx_rg, ctx_hg, ctx_m2,
                                           hg_norm_w[i], m2_norm_w[i], w_out[i])
    return rmsnorm(xl, final_norm_w)
```

```python
import functools
import math

import jax
import jax.numpy as jnp
import numpy as np
from jax import lax
from jax.experimental import pallas as pl
from jax.experimental.pallas import tpu as pltpu

D_MODEL = 1024
DEPTH = 2
CTX_LEN = 256
GRID_W = 64
W_BR = 512
NORM_EPS = 1e-6

HY_ORDER = 2
HY_EMB = 33
HY_BANDS = 16
HY_HID = 64
HY_SHORT = 3
HY_MIN_DECAY = math.log(1e-2) / 1.5
HY_MAX_DECAY = math.log(1e-2) / 0.3
HY_EPS = 1e-6

RG_HEADS = 8
RG_HD = 64
RG_C = 8.0
RG_CONV = 4

HG_DK = 128
HG_HEADS = 4
HG_CHUNK = 64

M2_HD = 64
M2_HEADS = 8
M2_GROUPS = 2
M2_STATE = 128
M2_CONV = 4
M2_CHUNK = 64
M2_XBC = W_BR + 2 * M2_GROUPS * M2_STATE

_OFF_HY, _OFF_HYG, _OFF_RGX, _OFF_RGG = 0, 1536, 2048, 2560
_OFF_HGQ, _OFF_FF, _OFF_FB, _OFF_HI, _OFF_HGG = 3072, 3584, 4096, 4608, 5120
_OFF_XBC, _OFF_DT, _OFF_M2G = 5632, 6656, 6664
MAIN_COLS = 11 * W_BR
DT_PAD = 128
VMEM_LIMIT = 56 * 1024 * 1024

_f32 = jnp.float32
_bf16 = jnp.bfloat16


def _cparams(sem):
    return pltpu.CompilerParams(dimension_semantics=sem, vmem_limit_bytes=VMEM_LIMIT)


def _mod_body(c_ref, w_ref, b_ref, o_ref):
    c = c_ref[...]
    s = c * jax.nn.sigmoid(c)
    o_ref[...] = jnp.dot(s, w_ref[...], precision=lax.Precision.HIGHEST,
                         preferred_element_type=_f32) + b_ref[...]


def _modulation(cond8, w_mod_i, b_mod_i):
    tn = 512
    return pl.pallas_call(
        _mod_body,
        grid=(3 * D_MODEL // tn,),
        in_specs=[pl.BlockSpec((8, D_MODEL), lambda j: (0, 0)),
                  pl.BlockSpec((D_MODEL, tn), lambda j: (0, j)),
                  pl.BlockSpec((1, tn), lambda j: (0, j))],
        out_specs=pl.BlockSpec((8, tn), lambda j: (0, j)),
        out_shape=jax.ShapeDtypeStruct((8, 3 * D_MODEL), _f32),
        compiler_params=_cparams(("arbitrary",)),
        name="modulation",
    )(cond8, w_mod_i, b_mod_i.reshape(1, -1))


def _modulated_norm(x, nw, sc, sh):
    y = x * lax.rsqrt(jnp.mean(x * x, axis=-1, keepdims=True) + NORM_EPS)
    return (y * nw) * (1.0 + sc) + sh


def _inproj_main_body(x_ref, sc_ref, sh_ref, nw_ref, w_ref, wdt_ref, u_ref, dt_ref, h_scr):
    j = pl.program_id(2)

    @pl.when(j == 0)
    def _():
        h = _modulated_norm(x_ref[0], nw_ref[...], sc_ref[0], sh_ref[0]).astype(_bf16)
        h_scr[...] = h
        dt_ref[0] = jnp.dot(h, wdt_ref[...], preferred_element_type=_f32)

    u_ref[0] = jnp.dot(h_scr[...], w_ref[...], preferred_element_type=_f32)


def _inproj_hy_body(x_ref, sc_ref, sh_ref, nw_ref, w_ref, u_ref, h_scr):
    j = pl.program_id(2)

    @pl.when(j == 0)
    def _():
        h_scr[...] = _modulated_norm(x_ref[0], nw_ref[...], sc_ref[0], sh_ref[0]).astype(_bf16)

    u_ref[0] = lax.dot_general(w_ref[...], h_scr[...], (((1,), (1,)), ((), ())),
                               preferred_element_type=_f32)


def _in_projection(x, sc, sh, nw, w_main, w_dt, w_hyT):
    bsz, n, _ = x.shape
    tm = min(n, 512)
    common = [pl.BlockSpec((1, tm, D_MODEL), lambda b, i, j: (b, i, 0)),
              pl.BlockSpec((1, 1, D_MODEL), lambda b, i, j: (b, 0, 0)),
              pl.BlockSpec((1, 1, D_MODEL), lambda b, i, j: (b, 0, 0)),
              pl.BlockSpec((1, D_MODEL), lambda b, i, j: (0, 0))]
    u_main, dt = pl.pallas_call(
        _inproj_main_body,
        grid=(bsz, n // tm, MAIN_COLS // W_BR),
        in_specs=common + [pl.BlockSpec((D_MODEL, W_BR), lambda b, i, j: (0, j)),
                           pl.BlockSpec((D_MODEL, DT_PAD), lambda b, i, j: (0, 0))],
        out_specs=[pl.BlockSpec((1, tm, W_BR), lambda b, i, j: (b, i, j)),
                   pl.BlockSpec((1, tm, DT_PAD), lambda b, i, j: (b, i, 0))],
        out_shape=[jax.ShapeDtypeStruct((bsz, n, MAIN_COLS), _f32),
                   jax.ShapeDtypeStruct((bsz, n, DT_PAD), _f32)],
        scratch_shapes=[pltpu.VMEM((tm, D_MODEL), _bf16)],
        compiler_params=_cparams(("arbitrary", "arbitrary", "arbitrary")),
        name="inproj_main",
    )(x, sc, sh, nw, w_main, w_dt)
    u_hyT = pl.pallas_call(
        _inproj_hy_body,
        grid=(bsz, n // tm, 3),
        in_specs=common + [pl.BlockSpec((W_BR, D_MODEL), lambda b, i, j: (j, 0))],
        out_specs=pl.BlockSpec((1, W_BR, tm), lambda b, i, j: (b, j, i)),
        out_shape=jax.ShapeDtypeStruct((bsz, 3 * W_BR, n), _f32),
        scratch_shapes=[pltpu.VMEM((tm, D_MODEL), _bf16)],
        compiler_params=_cparams(("arbitrary", "arbitrary", "arbitrary")),
        name="inproj_hyena",
    )(x, sc, sh, nw, w_hyT)
    return u_main, dt, u_hyT


def _silu(v):
    return v * jax.nn.sigmoid(v)


def _merge_body(x_ref, g_ref, yhy_ref, ghy_ref, yrg_ref, grg_ref, yhg_ref, ghg_ref,
                ym2_ref, gm2_ref, hgw_ref, m2w_ref, wout_ref, fw_ref, o_ref, *, final):
    b_hy = yhy_ref[0].T * _silu(ghy_ref[0])
    b_rg = yrg_ref[0] * _silu(grg_ref[0])
    yhg = yhg_ref[0]
    parts = []
    for h in range(HG_HEADS):
        o = yhg[:, h * HG_DK:(h + 1) * HG_DK]
        parts.append(o * lax.rsqrt(jnp.mean(o * o, axis=-1, keepdims=True) + NORM_EPS))
    b_hg = jnp.concatenate(parts, axis=-1) * hgw_ref[...] * _silu(ghg_ref[0])
    m2 = ym2_ref[0] * _silu(gm2_ref[0])
    gw = W_BR // M2_GROUPS
    parts = []
    for g in range(M2_GROUPS):
        o = m2[:, g * gw:(g + 1) * gw]
        parts.append(o * lax.rsqrt(jnp.mean(o * o, axis=-1, keepdims=True) + NORM_EPS))
    b_m2 = jnp.concatenate(parts, axis=-1) * m2w_ref[...]
    acc = None
    for k, br in enumerate((b_hy, b_rg, b_hg, b_m2)):
        d = jnp.dot(br.astype(_bf16), wout_ref[k * W_BR:(k + 1) * W_BR, :],
                    preferred_element_type=_f32)
        acc = d if acc is None else acc + d
    xn = x_ref[0] + g_ref[0] * acc
    if final:
        xn = xn * lax.rsqrt(jnp.mean(xn * xn, axis=-1, keepdims=True) + NORM_EPS) * fw_ref[...]
    o_ref[0] = xn


def _merge(x, g, y_hyT, u_main, y_rg, y_hg, y_m2, hg_w, m2_w, w_out, final_w, final):
    bsz, n, _ = x.shape
    tm = min(n, 256)
    col = lambda c: pl.BlockSpec((1, tm, W_BR), lambda b, i, c=c: (b, i, c))
    tok = pl.BlockSpec((1, tm, W_BR), lambda b, i: (b, i, 0))
    vec = lambda w: pl.BlockSpec((1, w), lambda b, i: (0, 0))
    return pl.pallas_call(
        functools.partial(_merge_body, final=final),
        grid=(bsz, n // tm),
        in_specs=[pl.BlockSpec((1, tm, D_MODEL), lambda b, i: (b, i, 0)),
                  pl.BlockSpec((1, 1, D_MODEL), lambda b, i: (b, 0, 0)),
                  pl.BlockSpec((1, W_BR, tm), lambda b, i: (b, 0, i)), col(0),
                  tok, col(2), tok, col(7), tok, col(10),
                  vec(W_BR), vec(W_BR),
                  pl.BlockSpec((4 * W_BR, D_MODEL), lambda b, i: (0, 0)),
                  vec(D_MODEL)],
        out_specs=pl.BlockSpec((1, tm, D_MODEL), lambda b, i: (b, i, 0)),
        out_shape=jax.ShapeDtypeStruct((bsz, n, D_MODEL), _f32),
        compiler_params=_cparams(("arbitrary", "arbitrary")),
        name="merge_final" if final else "merge",
    )(x, g, y_hyT, u_main, y_rg, u_main, y_hg, u_main, y_m2, u_main,
      hg_w.reshape(1, -1), m2_w.reshape(1, -1), w_out, final_w.reshape(1, -1))


def _dwconv(x, w, b, pad_lo, pad_hi):
    y = lax.conv_general_dilated(x, w.astype(x.dtype)[:, None, :], window_strides=(1,),
                                 padding=[(pad_lo, pad_hi)],
                                 dimension_numbers=('NWC', 'WIO', 'NWC'),
                                 feature_group_count=x.shape[-1])
    return y + b.astype(x.dtype)


def _to_colmajor(a, rows):
    b, n, ch = a.shape
    return a.reshape(b, rows, GRID_W, ch).transpose(0, 2, 1, 3).reshape(b, n, ch)


def _from_colmajor(a, rows):
    b, n, ch = a.shape
    return a.reshape(b, GRID_W, rows, ch).transpose(0, 2, 1, 3).reshape(b, n, ch)


def _bidir(dir_fn, ctx_fwd, lat_fwd, ctx_bwd, lat_bwd, state0, p_fwd, p_bwd):
    rev = lambda t: tuple(jnp.flip(a, 1) for a in t)
    yc_f, s_f = dir_fn(*ctx_fwd, state0, *p_fwd)
    yl_f, _ = dir_fn(*lat_fwd, s_f, *p_fwd)
    yc_b, s_b = dir_fn(*rev(ctx_bwd), state0, *p_bwd)
    yl_b, _ = dir_fn(*rev(lat_bwd), s_b, *p_bwd)
    return yc_f + jnp.flip(yc_b, 1), yl_f + jnp.flip(yl_b, 1)


def _hyena_filters(n, w1, b1, w2, b2, w3, freq):
    t = jnp.linspace(0.0, 1.0, n, dtype=_f32)[:, None]
    bands = jnp.linspace(1e-4, HY_BANDS - 1, HY_BANDS, dtype=_f32)[None]
    ang = (2.0 * math.pi / n) * jnp.arange(n, dtype=_f32)[:, None] * bands
    z = jnp.concatenate([t, jnp.cos(ang), -jnp.sin(ang)], axis=-1)
    h = jnp.sin(freq * (z @ w1 + b1))
    h = jnp.sin(freq * (h @ w2 + b2))
    h = (h @ w3).reshape(n, HY_ORDER, 2, W_BR)
    deltas = jnp.abs(jnp.linspace(HY_MIN_DECAY, HY_MAX_DECAY, W_BR, dtype=_f32))
    return h * jnp.exp(-t[:, :, None, None] * deltas)


def _two_sided_kernel(h_fwd, h_bwd):
    k = jnp.concatenate([h_fwd, jnp.zeros_like(h_fwd[:1]), jnp.flip(h_bwd[1:], 0)], axis=0)
    return k / (jnp.sum(jnp.abs(k), axis=0, keepdims=True) + HY_EPS)


def _fft_conv(u, k):
    n = u.shape[1]
    y = jnp.fft.irfft(jnp.fft.rfft(u, n=2 * n, axis=1) * jnp.fft.rfft(k, axis=0), n=2 * n, axis=1)
    return y[:, :n]


def _hyena_mix(u, conv_w, conv_b, w1, b1, w2, b2, w3, freq, skip):
    n = u.shape[1]
    u = _dwconv(u, conv_w, conv_b, HY_SHORT // 2, HY_SHORT // 2)
    v, x1, x2 = jnp.split(u, 3, axis=-1)
    h = _hyena_filters(n, w1, b1, w2, b2, w3, freq)
    z = v
    for o, xg in enumerate((x1, x2)):
        k = _two_sided_kernel(h[:, o, 0], h[:, o, 1])
        z = xg * (_fft_conv(z, k) + skip[o] * z)
    return z


def _lin_combine(e1, e2):
    a1, b1 = e1
    a2, b2 = e2
    return a1 * a2, a2 * b1 + b2


def _rglru_dir(x, h0, conv_w, conv_b, wa, ba, wx, bx, lam):
    bn, n, _ = x.shape
    xc = _dwconv(x, conv_w, conv_b, RG_CONV - 1, 0)
    xh = xc.reshape(bn, n, RG_HEADS, RG_HD)
    r = jax.nn.sigmoid(jnp.einsum('blhi,hij->blhj', xh, wa).reshape(bn, n, W_BR) + ba)
    gi = jax.nn.sigmoid(jnp.einsum('blhi,hij->blhj', xh, wx).reshape(bn, n, W_BR) + bx)
    log_a = -RG_C * r * jax.nn.softplus(-lam)
    a = jnp.exp(log_a)
    b = jnp.sqrt(-jnp.expm1(2.0 * log_a)) * (gi * xc)
    a_cum, h = lax.associative_scan(_lin_combine, (a, b), axis=1)
    h = h + a_cum * h0[:, None]
    return h, h[:, -1]


def _gla_chunked(q, k, v, log_f, s0):
    bn, n, nh, _ = q.shape
    dv = v.shape[-1]
    nc = n // HG_CHUNK
    to_chunks = lambda a: a.reshape(bn, nc, HG_CHUNK, nh, a.shape[-1]).transpose(1, 0, 3, 2, 4)
    mask = jnp.tril(jnp.ones((HG_CHUNK, HG_CHUNK), dtype=bool))[:, :, None]

    def step(s, inp):
        qc, kc, vc, lf = inp
        g = jnp.cumsum(lf, axis=2)
        o_inter = jnp.einsum('bhtd,bhde->bhte', qc * jnp.exp(g), s)
        diff = g[:, :, :, None, :] - g[:, :, None, :, :]
        decay = jnp.exp(jnp.where(mask, diff, -jnp.inf))
        att = jnp.einsum('bhtd,bhsd,bhtsd->bhts', qc, kc, decay)
        o = o_inter + jnp.einsum('bhts,bhse->bhte', att, vc)
        g_last = g[:, :, -1]
        s = jnp.exp(g_last)[..., None] * s + jnp.einsum('bhsd,bhse->bhde', kc * jnp.exp(g_last[:, :, None] - g), vc)
        return s, o

    s_last, o = lax.scan(step, s0, (to_chunks(q), to_chunks(k), to_chunks(v), to_chunks(log_f)))
    return o.transpose(1, 0, 3, 2, 4).reshape(bn, n, nh, dv), s_last


def _hgrn2_dir(q, f_logit, v, s0, lb):
    bn, n, _ = q.shape
    z = f_logit
    f = lb + (1.0 - lb) * jax.nn.sigmoid(z)
    k = (1.0 - lb) * jax.nn.sigmoid(-z)
    heads = lambda a: a.reshape(bn, n, HG_HEADS, HG_DK)
    return _gla_chunked(heads(q * HG_DK ** -0.5), heads(k), heads(v), heads(jnp.log(f)), s0)


def _segsum(a):
    cs = jnp.cumsum(a, axis=-1)
    t = a.shape[-1]
    mask = jnp.tril(jnp.ones((t, t), dtype=bool))
    return jnp.where(mask, cs[..., :, None] - cs[..., None, :], -jnp.inf)


def _ssd_chunked(xdt, adt, bm, cm, h0):
    bn, n, nh, hp = xdt.shape
    nc = n // M2_CHUNK
    X = xdt.reshape(bn, nc, M2_CHUNK, nh, hp)
    Bc = bm.reshape(bn, nc, M2_CHUNK, nh, M2_STATE)
    Cc = cm.reshape(bn, nc, M2_CHUNK, nh, M2_STATE)
    A = adt.reshape(bn, nc, M2_CHUNK, nh).transpose(0, 3, 1, 2)
    A_cum = jnp.cumsum(A, axis=-1)
    y_diag = jnp.einsum('bclhn,bcshn,bhcls,bcshp->bclhp', Cc, Bc, jnp.exp(_segsum(A)), X)
    decay_states = jnp.exp(A_cum[..., -1:] - A_cum)
    states = jnp.einsum('bclhn,bhcl,bclhp->bchpn', Bc, decay_states, X)
    states = jnp.concatenate([h0[:, None], states], axis=1)
    chunk_decay = jnp.exp(_segsum(jnp.pad(A_cum[..., -1], ((0, 0), (0, 0), (1, 0)))))
    states = jnp.einsum('bhzc,bchpn->bzhpn', chunk_decay, states)
    y_off = jnp.einsum('bclhn,bchpn,bhcl->bclhp', Cc, states[:, :-1], jnp.exp(A_cum))
    return (y_diag + y_off).reshape(bn, n, nh, hp), states[:, -1]


def _ssd_dir(xbc, dt_raw, h0, conv_w, conv_b, dt_bias, a_log, d_skip):
    bn, n, _ = xbc.shape
    xbc = jax.nn.silu(_dwconv(xbc, conv_w, conv_b, M2_CONV - 1, 0))
    xs, bm, cm = jnp.split(xbc, [W_BR, W_BR + M2_GROUPS * M2_STATE], axis=-1)
    xs = xs.reshape(bn, n, M2_HEADS, M2_HD)
    rep = M2_HEADS // M2_GROUPS
    bm = jnp.repeat(bm.reshape(bn, n, M2_GROUPS, M2_STATE), rep, axis=2)
    cm = jnp.repeat(cm.reshape(bn, n, M2_GROUPS, M2_STATE), rep, axis=2)
    dt = jax.nn.softplus(dt_raw + dt_bias)
    a = -jnp.exp(a_log)
    y, h_last = _ssd_chunked(xs * dt[..., None], dt * a, bm, cm, h0)
    return y + d_skip[:, None] * xs, h_last


def kernel(x, c, ctx, c_ctx, w_mod, b_mod, norm_w, w_in, w_out, hy_conv_w, hy_conv_b, hy_w1, hy_b1, hy_w2, hy_b2, hy_w3, hy_freq, hy_skip, rg_conv_w, rg_conv_b, rg_wa, rg_ba, rg_wx, rg_bx, rg_lam, hg_lb, hg_norm_w, m2_conv_w, m2_conv_b, m2_dt_bias, m2_a_log, m2_d, m2_norm_w, final_norm_w):
    bsz, n_lat, _ = x.shape
    rows = n_lat // GRID_W
    lb_all = jnp.cumsum(jax.nn.softmax(hg_lb, axis=0), axis=0)
    lb_all = lb_all - lb_all[:1]
    cond8 = jnp.concatenate([c, c_ctx[None], jnp.zeros((8 - bsz - 1, D_MODEL), _f32)], axis=0)
    rg0 = jnp.zeros((bsz, W_BR), _f32)
    hg0 = jnp.zeros((bsz, HG_HEADS, HG_DK, HG_DK), _f32)
    m20 = jnp.zeros((bsz, M2_HEADS, M2_HD, M2_STATE), _f32)
    xl, xc = x, ctx
    for i in range(DEPTH):
        need_ctx = i < DEPTH - 1
        final = i == DEPTH - 1
        mod = _modulation(cond8, w_mod[i], b_mod[i])
        sh_l, sc_l, g_l = [mod[:bsz, k * D_MODEL:(k + 1) * D_MODEL][:, None] for k in range(3)]
        sh_c, sc_c, g_c = [jnp.broadcast_to(mod[bsz:bsz + 1, k * D_MODEL:(k + 1) * D_MODEL][:, None],
                                            (bsz, 1, D_MODEL)) for k in range(3)]
        wi = w_in[i]
        w_main = jnp.concatenate([wi[:, _OFF_HYG:_OFF_DT], wi[:, _OFF_M2G:]], axis=1).astype(_bf16)
        w_dt = jnp.pad(wi[:, _OFF_DT:_OFF_M2G], ((0, 0), (0, DT_PAD - M2_HEADS))).astype(_bf16)
        w_hyT = wi[:, :_OFF_HYG].T.astype(_bf16)
        nw = norm_w[i].reshape(1, -1)
        w_out_b = w_out[i].astype(_bf16)

        ul_main, ul_dt, ul_hyT = _in_projection(xl, sc_l, sh_l, nw, w_main, w_dt, w_hyT)
        uc_main, uc_dt, uc_hyT = _in_projection(xc, sc_c, sh_c, nw, w_main, w_dt, w_hyT)
        blk = lambda u, k, w=1: u[:, :, k * W_BR:(k + w) * W_BR]

        hy_p = (hy_conv_w[i], hy_conv_b[i], hy_w1[i], hy_b1[i], hy_w2[i], hy_b2[i], hy_w3[i], hy_freq[i], hy_skip[i])
        lat_hy = _hyena_mix(ul_hyT.transpose(0, 2, 1), *hy_p)

        rg_f = (rg_conv_w[i, 0], rg_conv_b[i, 0], rg_wa[i, 0], rg_ba[i, 0], rg_wx[i, 0], rg_bx[i, 0], rg_lam[i, 0])
        rg_b = (rg_conv_w[i, 1], rg_conv_b[i, 1], rg_wa[i, 1], rg_ba[i, 1], rg_wx[i, 1], rg_bx[i, 1], rg_lam[i, 1])
        ctx_rg, lat_rg = _bidir(_rglru_dir, (blk(uc_main, 1),), (blk(ul_main, 1),), (blk(uc_main, 1),),
                                (blk(ul_main, 1),), rg0, rg_f, rg_b)

        ctx_hg, lat_hg = _bidir(_hgrn2_dir, (blk(uc_main, 3), blk(uc_main, 4), blk(uc_main, 6)),
                                (blk(ul_main, 3), blk(ul_main, 4), blk(ul_main, 6)),
                                (blk(uc_main, 3), blk(uc_main, 5), blk(uc_main, 6)),
                                (blk(ul_main, 3), blk(ul_main, 5), blk(ul_main, 6)), hg0,
                                (lb_all[i, 0],), (lb_all[i, 1],))

        m2_f = (m2_conv_w[i, 0], m2_conv_b[i, 0], m2_dt_bias[i, 0], m2_a_log[i, 0], m2_d[i, 0])
        m2_b = (m2_conv_w[i, 1], m2_conv_b[i, 1], m2_dt_bias[i, 1], m2_a_log[i, 1], m2_d[i, 1])
        m2_ctx_in = (blk(uc_main, 8, 2), uc_dt[:, :, :M2_HEADS])
        m2_lat_in = (_to_colmajor(blk(ul_main, 8, 2), rows), _to_colmajor(ul_dt[:, :, :M2_HEADS], rows))
        ctx_m2, lat_m2 = _bidir(_ssd_dir, m2_ctx_in, m2_lat_in, m2_ctx_in, m2_lat_in, m20, m2_f, m2_b)
        lat_m2 = _from_colmajor(lat_m2.reshape(bsz, n_lat, W_BR), rows)

        xl_new = _merge(xl, g_l, lat_hy.transpose(0, 2, 1), ul_main, lat_rg,
                        lat_hg.reshape(bsz, n_lat, W_BR), lat_m2,
                        hg_norm_w[i], m2_norm_w[i], w_out_b, final_norm_w, final)
        if need_ctx:
            ctx_hy = _hyena_mix(uc_hyT.transpose(0, 2, 1), *hy_p)
            xc = _merge(xc, g_c, ctx_hy.transpose(0, 2, 1), uc_main, ctx_rg,
                        ctx_hg.reshape(bsz, CTX_LEN, W_BR), ctx_m2.reshape(bsz, CTX_LEN, W_BR),
                        hg_norm_w[i], m2_norm_w[i], w_out_b, final_norm_w, False)
        xl = xl_new
    return xl
```
